```python
import jax
import jax.numpy as jnp
from jax import lax
import numpy as np

D_MODEL = 1024
BATCH = 8
SEQ = 2048
DEPTH = 2
DEC_BATCH = 128
DEC_SEQ = 8
PAST_LEN = 16384
PAGE_SIZE = 128

N_META = 16
N_EVEN = (DEPTH + 1) // 2
N_ODD = DEPTH // 2
CHUNK = 64
CONV_W = 4
LN_EPS = 1e-5
RMS_EPS = 1e-6
ALPHA = (2.0 * DEPTH) ** 0.25
BETA = (8.0 * DEPTH) ** -0.25
D_FF = 4 * D_MODEL
SSD_HEADDIM = 64
SSD_DINNER = D_MODEL
SSD_HEADS = SSD_DINNER // SSD_HEADDIM
SSD_GROUPS = 2
SSD_HPG = SSD_HEADS // SSD_GROUPS
SSD_DSTATE = 128
SSD_CONV_DIM = SSD_DINNER + 2 * SSD_GROUPS * SSD_DSTATE
RG_WIDTH = D_MODEL
RG_BLOCKS = 8
RG_BW = RG_WIDTH // RG_BLOCKS
RG_C = 8.0
GLA_HEADS = 4
GLA_DK = D_MODEL // 2 // GLA_HEADS
GLA_DV = D_MODEL // GLA_HEADS
GLA_RANK = 16
GLA_GATE_NORM = 16.0
HGRN_DK = 128
HGRN_HEADS = D_MODEL // HGRN_DK
HGRN_DV = HGRN_DK
EVEN_SIZES = (SSD_DINNER, SSD_CONV_DIM, SSD_HEADS, RG_WIDTH, RG_WIDTH)
ODD_SIZES = (GLA_HEADS * GLA_DK, GLA_HEADS * GLA_DK, GLA_HEADS * GLA_DV, GLA_HEADS * GLA_DV, GLA_RANK,
             HGRN_HEADS * HGRN_DK, HGRN_HEADS * HGRN_DK, HGRN_HEADS * HGRN_DV, HGRN_HEADS * HGRN_DV)
EVEN_IN = sum(EVEN_SIZES)
ODD_IN = sum(ODD_SIZES)
EVEN_SPLIT = [int(v) for v in np.cumsum(EVEN_SIZES)[:-1]]
ODD_SPLIT = [int(v) for v in np.cumsum(ODD_SIZES)[:-1]]
EVEN_MIX = SSD_DINNER + RG_WIDTH
ODD_MIX = GLA_HEADS * GLA_DV + HGRN_HEADS * HGRN_DV
F32 = jnp.float32

kernel_name = 'hybrid_ssd_rglru_gla_hgrn2_step'


def _layer_norm(x, g, b):
    xf = x.astype(F32)
    mu = jnp.mean(xf, -1, keepdims=True)
    var = jnp.mean(jnp.square(xf - mu), -1, keepdims=True)
    return ((xf - mu) * lax.rsqrt(var + LN_EPS) * g.astype(F32) + b.astype(F32)).astype(x.dtype)


def _rms_norm(x, g):
    xf = x.astype(F32)
    return (xf * lax.rsqrt(jnp.mean(jnp.square(xf), -1, keepdims=True) + RMS_EPS) * g.astype(F32)).astype(x.dtype)


def _causal_dwconv(u, buf, w, b):
    L = u.shape[1]
    full = jnp.concatenate([buf.astype(u.dtype), u], axis=1)
    out = b + sum(full[:, t:t + L] * w[t] for t in range(CONV_W))
    return out, full[:, L:]


def _to_chunks(a, q):
    bsz, L = a.shape[:2]
    n = -(-L // q)
    a = jnp.pad(a, [(0, 0), (0, n * q - L)] + [(0, 0)] * (a.ndim - 2))
    return jnp.moveaxis(a.reshape((bsz, n, q) + a.shape[2:]), 1, 0)


def _from_chunks(a, L):
    a = jnp.moveaxis(a, 0, 1)
    return a.reshape((a.shape[0], a.shape[1] * a.shape[2]) + a.shape[3:])[:, :L]


def _run_segments(fn, seg_lens, arrays, state):
    outs, start = [], 0
    for n in seg_lens:
        o, state = fn(*[a[:, start:start + n] for a in arrays], state)
        outs.append(o)
        start += n
    return jnp.concatenate(outs, axis=1), state


def _ssd_chunked(x, dt, log_a, bm, cm, s0):
    L = x.shape[1]
    q = min(CHUNK, L)
    xd = _to_chunks(x.astype(F32) * dt[..., None], q)
    b = jnp.cumsum(_to_chunks(log_a, q), axis=2)
    bm = _to_chunks(bm.astype(F32), q)
    cm = _to_chunks(cm.astype(F32), q)
    causal = jnp.tril(jnp.ones((q, q), dtype=bool))[:, :, None, None]
    seg = b[:, :, :, None] - b[:, :, None, :]
    decay = jnp.exp(jnp.where(causal, seg, -jnp.inf))
    cb = jnp.einsum('nbtgN,nbsgN->nbtsg', cm, bm)
    y_intra = jnp.einsum('nbtsgh,nbsghp->nbtghp', cb[..., None] * decay, xd)
    b_last = b[:, :, -1]
    to_end = jnp.exp(b_last[:, :, None] - b)

    def step(s, inp):
        c_c, eb_c, b_c, te_c, xd_c, bl_c = inp
        y = jnp.einsum('btgN,bghpN->btghp', c_c, s) * eb_c[..., None]
        s = s * jnp.exp(bl_c)[..., None, None] + jnp.einsum('bsgN,bsghp->bghpN', b_c, xd_c * te_c[..., None])
        return s, y

    s_fin, y_inter = lax.scan(step, s0, (cm, jnp.exp(b), bm, to_end, xd, b_last))
    return _from_chunks(y_intra + y_inter, L), s_fin


def _gla_chunked(qv, kv, vv, log_f, s0):
    L = qv.shape[1]
    c = min(CHUNK, L)
    qv, kv, vv, log_f = (_to_chunks(t.astype(F32), c) for t in (qv, kv, vv, log_f))
    b = jnp.cumsum(log_f, axis=2)
    b_ref = b[:, :, c // 2][:, :, None]
    scores = jnp.einsum('nbthk,nbshk->nbhts', qv * jnp.exp(b - b_ref), kv * jnp.exp(b_ref - b))
    causal = jnp.tril(jnp.ones((c, c), dtype=bool))
    scores = jnp.where(causal, scores, 0.0)
    o_intra = jnp.einsum('nbhts,nbshv->nbthv', scores, vv)
    b_last = b[:, :, -1]

    def step(s, inp):
        q_c, k_c, v_c, bl_c = inp
        o = jnp.einsum('bthk,bhkv->bthv', q_c, s)
        s = s * jnp.exp(bl_c)[..., None] + jnp.einsum('bshk,bshv->bhkv', k_c, v_c)
        return s, o

    s_fin, o_inter = lax.scan(step, s0, (qv * jnp.exp(b), kv * jnp.exp(b_last[:, :, None] - b), vv, b_last))
    return _from_chunks(o_intra + o_inter, L), s_fin


def _rglru_scan(a, bterm, h0):
    bterm = bterm.at[:, 0].add(a[:, 0] * h0)

    def combine(left, right):
        return left[0] * right[0], right[0] * left[1] + right[1]

    _, hs = lax.associative_scan(combine, (a, bterm), axis=1)
    return hs, hs[:, -1]


def _even_mixer(h, segs, st, p, j):
    s_ssd, s_ssd_conv, s_rg, s_rg_conv = st
    bsz, L, _ = h.shape
    u = h @ p['w_in_even'][j]
    z, xbc, dt_raw, rg_gate, rg_x = jnp.split(u, EVEN_SPLIT, axis=-1)
    xbc, ssd_conv_new = _causal_dwconv(xbc, s_ssd_conv, p['ssd_conv_w'][j], p['ssd_conv_b'][j])
    xbc = jax.nn.silu(xbc)
    xs, bm, cm = jnp.split(xbc, [SSD_DINNER, SSD_DINNER + SSD_GROUPS * SSD_DSTATE], axis=-1)
    xs = xs.reshape(bsz, L, SSD_GROUPS, SSD_HPG, SSD_HEADDIM)
    bm = bm.reshape(bsz, L, SSD_GROUPS, SSD_DSTATE)
    cm = cm.reshape(bsz, L, SSD_GROUPS, SSD_DSTATE)
    dt = jax.nn.softplus(dt_raw.astype(F32) + p['ssd_dt_bias'][j].astype(F32)).reshape(bsz, L, SSD_GROUPS, SSD_HPG)
    a = -jnp.exp(p['ssd_a_log'][j].astype(F32)).reshape(SSD_GROUPS, SSD_HPG)
    s0 = s_ssd.astype(F32).reshape(bsz, SSD_GROUPS, SSD_HPG, SSD_HEADDIM, SSD_DSTATE)
    y, ssd_new = _run_segments(_ssd_chunked, segs, (xs, dt, dt * a, bm, cm), s0)
    y = y + p['ssd_d'][j].astype(F32).reshape(SSD_GROUPS, SSD_HPG, 1) * xs.astype(F32)
    y = y.reshape(bsz, L, SSD_DINNER) * jax.nn.silu(z.astype(F32))
    y_ssd = _rms_norm(y.reshape(bsz, L, SSD_GROUPS, -1),
                      p['ssd_norm_g'][j].reshape(SSD_GROUPS, -1)).reshape(bsz, L, SSD_DINNER)
    xr, rg_conv_new = _causal_dwconv(rg_x, s_rg_conv, p['rg_conv_w'][j], p['rg_conv_b'][j])
    xb = xr.reshape(bsz, L, RG_BLOCKS, RG_BW)
    r = jax.nn.sigmoid((jnp.einsum('blnc,ncd->blnd', xb, p['rg_wa'][j]).reshape(bsz, L, RG_WIDTH)
                        + p['rg_ba'][j]).astype(F32))
    i = jax.nn.sigmoid((jnp.einsum('blnc,ncd->blnd', xb, p['rg_wx'][j]).reshape(bsz, L, RG_WIDTH)
                        + p['rg_bx'][j]).astype(F32))
    log_a = -RG_C * r * jax.nn.softplus(-p['rg_lambda'][j].astype(F32))
    gated = jnp.sqrt(-jnp.expm1(2.0 * log_a)) * (i * xr.astype(F32))
    hs, rg_new = _rglru_scan(jnp.exp(log_a), gated, s_rg.astype(F32))
    y_rg = hs * jax.nn.gelu(rg_gate.astype(F32))
    mixed = jnp.concatenate([y_ssd, y_rg], axis=-1).astype(h.dtype) @ p['w_out_even'][j]
    ssd_new = ssd_new.reshape(bsz, SSD_HEADS, SSD_HEADDIM, SSD_DSTATE)
    return mixed, (ssd_new, ssd_conv_new, rg_new, rg_conv_new)


def _odd_mixer(h, segs, st, p, j, layer):
    s_gla, s_hgrn = st
    bsz, L, _ = h.shape
    u = h @ p['w_in_odd'][j]
    qg, kg, vg, gg, lrg, qh, fh, ih, gh = jnp.split(u, ODD_SPLIT, axis=-1)
    gla_h = lambda t, d: t.reshape(bsz, L, GLA_HEADS, d)
    log_f = jax.nn.log_sigmoid((lrg @ p['gla_wg2'][j] + p['gla_bg2'][j]).astype(F32)) / GLA_GATE_NORM
    o_g, gla_new = _run_segments(
        _gla_chunked, segs,
        (gla_h(qg * GLA_DK ** -0.5, GLA_DK), gla_h(kg, GLA_DK), gla_h(vg, GLA_DV), gla_h(log_f, GLA_DK)),
        s_gla.astype(F32))
    o_g = _rms_norm(o_g, p['gla_norm_g'][j]) * jax.nn.silu(gla_h(gg, GLA_DV).astype(F32))
    gamma = jax.nn.softmax(p['hgrn_lb_logits'].astype(F32), axis=0)
    lb = jnp.cumsum(gamma, axis=0)[layer] - gamma[0]
    fs = fh.astype(F32)
    log_fh = jnp.log(lb + (1.0 - lb) * jax.nn.sigmoid(fs))
    kh = (1.0 - lb) * jax.nn.sigmoid(-fs)
    hg_h = lambda t, d: t.reshape(bsz, L, HGRN_HEADS, d)
    o_h, hgrn_new = _run_segments(
        _gla_chunked, segs,
        (hg_h(jax.nn.silu(qh), HGRN_DK), hg_h(kh, HGRN_DK), hg_h(ih, HGRN_DV), hg_h(log_fh, HGRN_DK)),
        s_hgrn.astype(F32))
    o_h = _rms_norm(o_h, p['hgrn_norm_g'][j]) * jax.nn.silu(hg_h(gh, HGRN_DV).astype(F32))
    mixed = jnp.concatenate([o_g.reshape(bsz, L, -1), o_h.reshape(bsz, L, -1)], axis=-1).astype(h.dtype) @ p['w_out_odd'][j]
    return mixed, (gla_new, hgrn_new)


def _sq_relu_mlp(h, w1, w2):
    return jnp.square(jax.nn.relu(h @ w1)) @ w2


def _zero_states(bsz, dtype):
    return (jnp.zeros((N_EVEN, bsz, SSD_HEADS, SSD_HEADDIM, SSD_DSTATE), F32),
            jnp.zeros((N_EVEN, bsz, CONV_W - 1, SSD_CONV_DIM), dtype),
            jnp.zeros((N_EVEN, bsz, RG_WIDTH), F32),
            jnp.zeros((N_EVEN, bsz, CONV_W - 1, RG_WIDTH), dtype),
            jnp.zeros((N_ODD, bsz, GLA_HEADS, GLA_DK, GLA_DV), F32),
            jnp.zeros((N_ODD, bsz, HGRN_HEADS, HGRN_DK, HGRN_DV), F32))


def _trunk(h, segs, states, p):
    ssd, ssd_conv, rg, rg_conv, gla, hgrn = states
    n_ssd, n_ssd_conv, n_rg, n_rg_conv, n_gla, n_hgrn = [], [], [], [], [], []
    for layer in range(DEPTH):
        j = layer // 2
        if layer % 2 == 0:
            mixed, (a1, a2, a3, a4) = _even_mixer(h, segs, (ssd[j], ssd_conv[j], rg[j], rg_conv[j]), p, j)
            n_ssd.append(a1)
            n_ssd_conv.append(a2)
            n_rg.append(a3)
            n_rg_conv.append(a4)
        else:
            mixed, (a5, a6) = _odd_mixer(h, segs, (gla[j], hgrn[j]), p, j, layer)
            n_gla.append(a5)
            n_hgrn.append(a6)
        h = _layer_norm(ALPHA * h + mixed, p['ln1_g'][layer], p['ln1_b'][layer])
        h = _layer_norm(ALPHA * h + _sq_relu_mlp(h, p['mlp_w1'][layer], p['mlp_w2'][layer]),
                        p['ln2_g'][layer], p['ln2_b'][layer])
    return h, (jnp.stack(n_ssd), jnp.stack(n_ssd_conv), jnp.stack(n_rg), jnp.stack(n_rg_conv),
               jnp.stack(n_gla), jnp.stack(n_hgrn))


def setup_inputs(seed: int = 0) -> dict:
    key = jax.random.key(seed)
    ks = list(jax.random.split(key, 64))
    nrm = lambda shape, s: s * jax.random.normal(ks.pop(), shape, F32)
    uni = lambda shape, lo, hi: jax.random.uniform(ks.pop(), shape, F32, lo, hi)
    dt0 = jnp.exp(uni((N_EVEN, SSD_HEADS), float(np.log(1e-3)), float(np.log(1e-1))))
    lam_s = uni((N_EVEN, RG_WIDTH), 0.9, 0.999) ** (1.0 / RG_C)
    return {
        'x_prompt': nrm((BATCH, SEQ, D_MODEL), 1.0),
        'x_sample': nrm((DEC_BATCH, DEC_SEQ, D_MODEL), 1.0),
        'state_ssd': nrm((N_EVEN, DEC_BATCH, SSD_HEADS, SSD_HEADDIM, SSD_DSTATE), 0.1),
        'state_ssd_conv': nrm((N_EVEN, DEC_BATCH, CONV_W - 1, SSD_CONV_DIM), 1.0),
        'state_rglru': nrm((N_EVEN, DEC_BATCH, RG_WIDTH), 0.5),
        'state_rglru_conv': nrm((N_EVEN, DEC_BATCH, CONV_W - 1, RG_WIDTH), 1.0),
        'state_gla': nrm((N_ODD, DEC_BATCH, GLA_HEADS, GLA_DK, GLA_DV), 0.1),
        'state_hgrn': nrm((N_ODD, DEC_BATCH, HGRN_HEADS, HGRN_DK, HGRN_DV), 0.3),
        'meta_tokens': nrm((N_META, D_MODEL), 1.0),
        'w_in_even': nrm((N_EVEN, D_MODEL, EVEN_IN), D_MODEL ** -0.5),
        'ssd_conv_w': nrm((N_EVEN, CONV_W, SSD_CONV_DIM), CONV_W ** -0.5),
        'ssd_conv_b': nrm((N_EVEN, SSD_CONV_DIM), 0.01),
        'ssd_dt_bias': dt0 + jnp.log(-jnp.expm1(-dt0)),
        'ssd_a_log': jnp.log(uni((N_EVEN, SSD_HEADS), 1.0, 16.0)),
        'ssd_d': 1.0 + nrm((N_EVEN, SSD_HEADS), 0.01),
        'ssd_norm_g': 1.0 + nrm((N_EVEN, SSD_DINNER), 0.01),
        'rg_conv_w': nrm((N_EVEN, CONV_W, RG_WIDTH), CONV_W ** -0.5),
        'rg_conv_b': nrm((N_EVEN, RG_WIDTH), 0.01),
        'rg_wa': nrm((N_EVEN, RG_BLOCKS, RG_BW, RG_BW), RG_BW ** -0.5),
        'rg_ba': nrm((N_EVEN, RG_WIDTH), 0.01),
        'rg_wx': nrm((N_EVEN, RG_BLOCKS, RG_BW, RG_BW), RG_BW ** -0.5),
        'rg_bx': nrm((N_EVEN, RG_WIDTH), 0.01),
        'rg_lambda': jnp.log(lam_s) - jnp.log1p(-lam_s),
        'w_out_even': nrm((N_EVEN, EVEN_MIX, D_MODEL), EVEN_MIX ** -0.5 * BETA),
        'w_in_odd': nrm((N_ODD, D_MODEL, ODD_IN), D_MODEL ** -0.5),
        'gla_wg2': nrm((N_ODD, GLA_RANK, GLA_HEADS * GLA_DK), GLA_RANK ** -0.5),
        'gla_bg2': nrm((N_ODD, GLA_HEADS * GLA_DK), 0.01),
        'gla_norm_g': 1.0 + nrm((N_ODD, GLA_DV), 0.01),
        'hgrn_lb_logits': nrm((DEPTH, HGRN_HEADS * HGRN_DK), 0.1),
        'hgrn_norm_g': 1.0 + nrm((N_ODD, HGRN_DV), 0.01),
        'w_out_odd': nrm((N_ODD, ODD_MIX, D_MODEL), ODD_MIX ** -0.5 * BETA),
        'mlp_w1': nrm((DEPTH, D_MODEL, D_FF), D_MODEL ** -0.5),
        'mlp_w2': nrm((DEPTH, D_FF, D_MODEL), D_FF ** -0.5 * BETA),
        'ln1_g': 1.0 + nrm((DEPTH, D_MODEL), 0.01),
        'ln1_b': nrm((DEPTH, D_MODEL), 0.01),
        'ln2_g': 1.0 + nrm((DEPTH, D_MODEL), 0.01),
        'ln2_b': nrm((DEPTH, D_MODEL), 0.01),
    }


def reference(x_prompt, x_sample, state_ssd, state_ssd_conv, state_rglru, state_rglru_conv, state_gla,
              state_hgrn, meta_tokens, w_in_even, ssd_conv_w, ssd_conv_b, ssd_dt_bias, ssd_a_log, ssd_d,
              ssd_norm_g, rg_conv_w, rg_conv_b, rg_wa, rg_ba, rg_wx, rg_bx, rg_lambda, w_out_even, w_in_odd,
              gla_wg2, gla_bg2, gla_norm_g, hgrn_lb_logits, hgrn_norm_g, w_out_odd, mlp_w1, mlp_w2,
              ln1_g, ln1_b, ln2_g, ln2_b):
    p = {'w_in_even': w_in_even, 'ssd_conv_w': ssd_conv_w, 'ssd_conv_b': ssd_conv_b,
         'ssd_dt_bias': ssd_dt_bias, 'ssd_a_log': ssd_a_log, 'ssd_d': ssd_d, 'ssd_norm_g': ssd_norm_g,
         'rg_conv_w': rg_conv_w, 'rg_conv_b': rg_conv_b, 'rg_wa': rg_wa, 'rg_ba': rg_ba, 'rg_wx': rg_wx,
         'rg_bx': rg_bx, 'rg_lambda': rg_lambda, 'w_out_even': w_out_even, 'w_in_odd': w_in_odd,
         'gla_wg2': gla_wg2, 'gla_bg2': gla_bg2, 'gla_norm_g': gla_norm_g, 'hgrn_lb_logits': hgrn_lb_logits,
         'hgrn_norm_g': hgrn_norm_g, 'w_out_odd': w_out_odd, 'mlp_w1': mlp_w1, 'mlp_w2': mlp_w2,
         'ln1_g': ln1_g, 'ln1_b': ln1_b, 'ln2_g': ln2_g, 'ln2_b': ln2_b}
    bsz = x_prompt.shape[0]
    meta = jnp.broadcast_to(meta_tokens.astype(x_prompt.dtype)[None], (bsz, N_META, D_MODEL))
    h_p = jnp.concatenate([meta, x_prompt], axis=1)
    h_p, (p_ssd, p_ssd_conv, p_rglru, p_rglru_conv, p_gla, p_hgrn) = _trunk(
        h_p, (N_META, x_prompt.shape[1]), _zero_states(bsz, x_prompt.dtype), p)
    y_prompt = h_p[:, N_META:]
    y_sample, (s_ssd, s_ssd_conv, s_rglru, s_rglru_conv, s_gla, s_hgrn) = _trunk(
        x_sample, (x_sample.shape[1],),
        (state_ssd, state_ssd_conv, state_rglru, state_rglru_conv, state_gla, state_hgrn), p)
    return (y_prompt, y_sample, p_ssd, p_ssd_conv, p_rglru, p_rglru_conv, p_gla, p_hgrn,
            s_ssd, s_ssd_conv, s_rglru, s_rglru_conv, s_gla, s_hgrn)
```

```python
import functools

import jax
import jax.numpy as jnp
from jax import lax
from jax.experimental import pallas as pl
from jax.experimental.pallas import tpu as pltpu

F32 = jnp.float32
BF16 = jnp.bfloat16

D_MODEL = 1024
DEPTH = 2
N_META = 16
CONV_W = 4
LN_EPS = 1e-5
RMS_EPS = 1e-6
ALPHA = (2.0 * DEPTH) ** 0.25
D_FF = 4 * D_MODEL
SSD_HEADDIM = 64
SSD_DINNER = D_MODEL
SSD_HEADS = SSD_DINNER // SSD_HEADDIM
SSD_GROUPS = 2
SSD_HPG = SSD_HEADS // SSD_GROUPS
SSD_DSTATE = 128
SSD_BC = 2 * SSD_GROUPS * SSD_DSTATE
RG_WIDTH = D_MODEL
RG_BLOCKS = 8
RG_BW = RG_WIDTH // RG_BLOCKS
RG_C = 8.0
GLA_HEADS = 4
GLA_DK = D_MODEL // 2 // GLA_HEADS
GLA_DV = D_MODEL // GLA_HEADS
GLA_RANK = 16
GLA_GATE_NORM = 16.0
HGRN_DK = 128
HGRN_HEADS = D_MODEL // HGRN_DK
HGRN_DV = HGRN_DK

LANES = 128
SUBLANES = 8
VMEM_LIMIT = 56 * 1024 * 1024
NEG_BIG = -1e30

EVEN_COLS = 5120
E_Z, E_XS, E_RGG, E_RGX, E_BC, E_DT = 0, 1024, 2048, 3072, 4096, 4608
ODD_COLS = 7680
O_QG, O_KG, O_VG, O_GG, O_QH, O_FH, O_IH, O_GH, O_LR = 0, 512, 1024, 2048, 3072, 4096, 5120, 6144, 7168

SSD_CHUNK = 128
GLA_CHUNK = 64
RG_CHUNK = 128


def _params(*sem):
    return pltpu.CompilerParams(dimension_semantics=sem, vmem_limit_bytes=VMEM_LIMIT)


def _dot(a, b):
    return jnp.dot(a.astype(BF16), b.astype(BF16), preferred_element_type=F32)


def _dot_nt(a, b):
    return lax.dot_general(a.astype(BF16), b.astype(BF16), (((1,), (1,)), ((), ())),
                           preferred_element_type=F32)


def _softplus(x):
    return jnp.maximum(x, 0.0) + jnp.log1p(jnp.exp(-jnp.abs(x)))


def _silu(x):
    return x * jax.nn.sigmoid(x)


def _pad_rows(x, rows):
    if x.shape[0] == rows:
        return x
    return jnp.concatenate([x, jnp.zeros((rows - x.shape[0], x.shape[1]), x.dtype)], axis=0)


def _tri(n):
    r = lax.broadcasted_iota(jnp.int32, (n, n), 0)
    c = lax.broadcasted_iota(jnp.int32, (n, n), 1)
    return r >= c


def _cumsum_rows(tri_bf16, x):
    hi = x.astype(BF16)
    r1 = x - hi.astype(F32)
    mid = r1.astype(BF16)
    lo = (r1 - mid.astype(F32)).astype(BF16)
    d = lambda t: jnp.dot(tri_bf16, t, preferred_element_type=F32)
    return d(hi) + d(mid) + d(lo)


def _layer_norm(x, g, b):
    mu = jnp.mean(x, -1, keepdims=True)
    xc = x - mu
    var = jnp.mean(xc * xc, -1, keepdims=True)
    return xc * lax.rsqrt(var + LN_EPS) * g + b


def _conv_step(ext_ref, u, w, b, q):
    ext_ref[SUBLANES:SUBLANES + q, :] = u
    out = b
    for k in range(CONV_W):
        lo = SUBLANES - (CONV_W - 1) + k
        out = out + ext_ref[lo:lo + q, :] * w[k:k + 1, :]
    tail = ext_ref[q:q + SUBLANES, :]
    ext_ref[0:SUBLANES, :] = tail
    return out, tail[SUBLANES - (CONV_W - 1):, :]


def _conv_init(ext_ref, cs):
    ext_ref[0:SUBLANES, :] = jnp.concatenate(
        [jnp.zeros((SUBLANES - (CONV_W - 1), cs.shape[1]), F32), cs], axis=0)


def _matmul_kernel(x_ref, w_ref, o_ref, xb_ref):
    @pl.when(pl.program_id(1) == 0)
    def _():
        xb_ref[...] = x_ref[...].astype(BF16)

    o_ref[...] = jnp.dot(xb_ref[...], w_ref[...], preferred_element_type=F32)


def _in_proj(x, w, name):
    t, k = x.shape
    n = w.shape[1]
    tm = min(t, 1024)
    tn = 512
    return pl.pallas_call(
        _matmul_kernel,
        grid=(t // tm, n // tn),
        in_specs=[pl.BlockSpec((tm, k), lambda i, j: (i, 0)),
                  pl.BlockSpec((k, tn), lambda i, j: (0, j))],
        out_specs=pl.BlockSpec((tm, tn), lambda i, j: (i, j)),
        out_shape=jax.ShapeDtypeStruct((t, n), F32),
        scratch_shapes=[pltpu.VMEM((tm, k), BF16)],
        compiler_params=_params("parallel", "arbitrary"),
        name=name,
    )(x, w)


def _outproj_ln_kernel(h_ref, ya_ref, yb_ref, wa_ref, wb_ref, g_ref, b_ref, o_ref):
    mixed = _dot(ya_ref[...], wa_ref[...]) + _dot(yb_ref[...], wb_ref[...])
    o_ref[...] = _layer_norm(ALPHA * h_ref[...] + mixed, g_ref[...], b_ref[...])


def _outproj_ln(h, ya, yb, wa, wb, g, b, name):
    t, d = h.shape
    tm = min(t, 512)
    row = pl.BlockSpec((tm, d), lambda i: (i, 0))
    full = lambda a: pl.BlockSpec(a.shape, lambda i: (0,) * a.ndim)
    return pl.pallas_call(
        _outproj_ln_kernel,
        grid=(t // tm,),
        in_specs=[row, row, row, full(wa), full(wb), full(g), full(b)],
        out_specs=row,
        out_shape=jax.ShapeDtypeStruct((t, d), F32),
        compiler_params=_params("parallel"),
        name=name,
    )(h, ya, yb, wa, wb, g, b)


def _mlp_ln_kernel(h_ref, w1_ref, w2_ref, g_ref, b_ref, o_ref, hb_ref, acc_ref, *, nf):
    f = pl.program_id(1)

    @pl.when(f == 0)
    def _():
        hb_ref[...] = h_ref[...].astype(BF16)
        acc_ref[...] = jnp.zeros_like(acc_ref)

    hid = jnp.dot(hb_ref[...], w1_ref[...], preferred_element_type=F32)
    hid = jnp.square(jnp.maximum(hid, 0.0))
    acc_ref[...] += _dot(hid, w2_ref[...])

    @pl.when(f == nf - 1)
    def _():
        o_ref[...] = _layer_norm(ALPHA * h_ref[...] + acc_ref[...], g_ref[...], b_ref[...])


def _mlp_ln(h, w1, w2, g, b, name):
    t, d = h.shape
    ff = w1.shape[1]
    tm = min(t, 1024)
    tf = 512
    nf = ff // tf
    return pl.pallas_call(
        functools.partial(_mlp_ln_kernel, nf=nf),
        grid=(t // tm, nf),
        in_specs=[pl.BlockSpec((tm, d), lambda i, f: (i, 0)),
                  pl.BlockSpec((d, tf), lambda i, f: (0, f)),
                  pl.BlockSpec((tf, d), lambda i, f: (f, 0)),
                  pl.BlockSpec((1, d), lambda i, f: (0, 0)),
                  pl.BlockSpec((1, d), lambda i, f: (0, 0))],
        out_specs=pl.BlockSpec((tm, d), lambda i, f: (i, 0)),
        out_shape=jax.ShapeDtypeStruct((t, d), F32),
        scratch_shapes=[pltpu.VMEM((tm, d), BF16), pltpu.VMEM((tm, d), F32)],
        compiler_params=_params("parallel", "arbitrary"),
        name=name,
    )(h, w1, w2, g, b)


def _seq_call(kernel, *, name, u, row_off, nseq, seqlen, q, u_cols, states, shared_init, params,
              y_prev, state_out_shapes, scratch):
    nc = seqlen // q
    blk0 = row_off // q
    row_map = lambda w_off, w: (lambda s, c: (blk0 + s * nc + c, w_off // w))
    in_specs, args = [], []
    for off, w in u_cols:
        in_specs.append(pl.BlockSpec((q, w), row_map(off, w)))
        args.append(u)
    for st in states:
        blk = (1,) + st.shape[1:]
        nd = st.ndim
        if shared_init:
            in_specs.append(pl.BlockSpec(blk, lambda s, c, nd=nd: (0,) * nd))
        else:
            in_specs.append(pl.BlockSpec(blk, lambda s, c, nd=nd: (s,) + (0,) * (nd - 1)))
        args.append(st)
    for p in params:
        in_specs.append(pl.BlockSpec(p.shape, lambda s, c, nd=p.ndim: (0,) * nd))
        args.append(p)
    aliases = {}
    if y_prev is not None:
        in_specs.append(pl.BlockSpec(memory_space=pl.ANY))
        aliases = {len(args): 0}
        args.append(y_prev)
    out_shape = [jax.ShapeDtypeStruct((u.shape[0], D_MODEL), F32)]
    out_specs = [pl.BlockSpec((q, D_MODEL), row_map(0, D_MODEL))]
    for shp in state_out_shapes:
        out_shape.append(jax.ShapeDtypeStruct((nseq,) + shp, F32))
        out_specs.append(pl.BlockSpec((1,) + shp, lambda s, c, nd=len(shp): (s,) + (0,) * nd))
    return pl.pallas_call(
        functools.partial(kernel, q=q, nc=nc, has_prev=y_prev is not None),
        grid=(nseq, nc),
        in_specs=in_specs,
        out_specs=out_specs,
        out_shape=out_shape,
        scratch_shapes=scratch,
        input_output_aliases=aliases,
        compiler_params=_params("parallel", "arbitrary"),
        name=name,
    )(*args)


def _ssd_kernel(z_ref, xs_ref, bc_ref, dt_ref, cs_ref, s0_ref,
                cwx_ref, cwb_ref, cbx_ref, cbb_ref, dtb_ref, alog_ref, dexp_ref, ng_ref,
                *rest, q, nc, has_prev):
    if has_prev:
        rest = rest[1:]
    y_ref, csn_ref, sn_ref, extx_ref, extb_ref, st_ref = rest
    c = pl.program_id(1)
    qp = max(q, SSD_CHUNK)
    hd, ds = SSD_HEADDIM, SSD_DSTATE

    @pl.when(c == 0)
    def _():
        cs = cs_ref[0]
        _conv_init(extx_ref, cs[:, :SSD_DINNER])
        _conv_init(extb_ref, cs[:, SSD_DINNER:])
        st_ref[...] = s0_ref[0]

    xs, tx = _conv_step(extx_ref, xs_ref[...], cwx_ref[...], cbx_ref[...], q)
    bc, tb = _conv_step(extb_ref, bc_ref[...], cwb_ref[...], cbb_ref[...], q)
    csn_ref[0] = jnp.concatenate([tx, tb], axis=1)
    xs = _silu(xs)
    bc = _silu(bc)
    dt = _softplus(dt_ref[...] + dtb_ref[...])
    la = dt * (-jnp.exp(alog_ref[...]))

    xs_p = _pad_rows(xs, qp)
    bc_p = _pad_rows(bc, qp)
    dt_p = _pad_rows(dt, qp)
    la_p = _pad_rows(la, qp)

    causal = _tri(qp)
    tri_b = causal.astype(BF16)
    b = _cumsum_rows(tri_b, la_p)
    bt = b.T
    bl = b[qp - 1:qp, :]
    lane_lo = lax.broadcasted_iota(jnp.int32, (qp, LANES), 1) < hd

    cbm = []
    for g in range(SSD_GROUPS):
        bm_g = bc_p[:, g * ds:(g + 1) * ds]
        cm_g = bc_p[:, (SSD_GROUPS + g) * ds:(SSD_GROUPS + g + 1) * ds]
        cbm.append(_dot_nt(cm_g, bm_g))

    ys = []
    for j in range(SSD_HEADS // 2):
        g = (2 * j) // SSD_HPG
        bm_g = bc_p[:, g * ds:(g + 1) * ds]
        cm_g = bc_p[:, (SSD_GROUPS + g) * ds:(SSD_GROUPS + g + 1) * ds]
        xs_j = xs_p[:, j * LANES:(j + 1) * LANES]
        bcols, dcols, ms = [], [], []
        for h in (2 * j, 2 * j + 1):
            bcol = jnp.broadcast_to(b[:, h:h + 1], (qp, LANES))
            bcols.append(bcol)
            dcols.append(jnp.broadcast_to(dt_p[:, h:h + 1], (qp, LANES)))
            brow = jnp.broadcast_to(bt[h:h + 1, :], (qp, qp))
            seg = jnp.where(causal, bcol[:, :qp] - brow, NEG_BIG)
            ms.append(cbm[g] * jnp.exp(seg))
        b_pair = jnp.where(lane_lo, bcols[0], bcols[1])
        dt_pair = jnp.where(lane_lo, dcols[0], dcols[1])
        bl_pair = jnp.where(lane_lo[:1],
                            jnp.broadcast_to(bl[:, 2 * j:2 * j + 1], (1, LANES)),
                            jnp.broadcast_to(bl[:, 2 * j + 1:2 * j + 2], (1, LANES)))
        xd = xs_j * dt_pair
        xd_b = xd.astype(BF16)
        y_intra = jnp.where(lane_lo, _dot(ms[0], xd_b), _dot(ms[1], xd_b))
        s_pair = st_ref[j * LANES:(j + 1) * LANES, :]
        y_inter = _dot_nt(cm_g, s_pair) * jnp.exp(b_pair)
        ys.append(y_intra + y_inter + dexp_ref[:, j * LANES:(j + 1) * LANES] * xs_j)
        xdte = xd * jnp.exp(bl_pair - b_pair)
        upd = _dot(xdte.T, bm_g)
        for i, h in enumerate((2 * j, 2 * j + 1)):
            e_h = jnp.exp(jnp.broadcast_to(bl[:, h:h + 1], (hd, ds)))
            st_ref[h * hd:(h + 1) * hd, :] = (s_pair[i * hd:(i + 1) * hd, :] * e_h
                                              + upd[i * hd:(i + 1) * hd, :])

    y = jnp.concatenate(ys, axis=1)[:q]
    y = y * _silu(z_ref[...])
    gw = SSD_DINNER // SSD_GROUPS
    outs = []
    for g in range(SSD_GROUPS):
        yg = y[:, g * gw:(g + 1) * gw]
        outs.append(yg * lax.rsqrt(jnp.mean(yg * yg, -1, keepdims=True) + RMS_EPS))
    y_ref[...] = jnp.concatenate(outs, axis=1) * ng_ref[...]

    @pl.when(c == nc - 1)
    def _():
        sn_ref[0] = st_ref[...]


def _rg_kernel(gate_ref, x_ref, cs_ref, h0_ref,
               cw_ref, cb_ref, wax_ref, bax_ref, lam_ref,
               *rest, q, nc, has_prev):
    if has_prev:
        rest = rest[1:]
    y_ref, csn_ref, hn_ref, ext_ref, h_ref = rest
    c = pl.program_id(1)

    @pl.when(c == 0)
    def _():
        _conv_init(ext_ref, cs_ref[0])
        h_ref[...] = h0_ref[0]

    xr, tail = _conv_step(ext_ref, x_ref[...], cw_ref[...], cb_ref[...], q)
    csn_ref[0] = tail
    sp = _softplus(-lam_ref[...])
    a_parts, g_parts = [], []
    for n in range(RG_BLOCKS):
        xr_n = xr[:, n * RG_BW:(n + 1) * RG_BW]
        gts = _dot(xr_n, wax_ref[n]) + bax_ref[n]
        r = jax.nn.sigmoid(gts[:, :RG_BW])
        i = jax.nn.sigmoid(gts[:, RG_BW:])
        log_a = -RG_C * r * sp[:, n * RG_BW:(n + 1) * RG_BW]
        a_n = jnp.exp(log_a)
        a_parts.append(a_n)
        g_parts.append(jnp.sqrt(-jnp.tanh(log_a) * (a_n * a_n + 1.0)) * (i * xr_n))
    a = jnp.concatenate(a_parts, axis=1)
    g = jnp.concatenate(g_parts, axis=1)
    row = lax.broadcasted_iota(jnp.int32, (q, RG_WIDTH), 0)
    s = 1
    while s < q:
        m = row >= s
        g = g + jnp.where(m, a * pltpu.roll(g, s, 0), 0.0)
        a = jnp.where(m, a * pltpu.roll(a, s, 0), a)
        s *= 2
    h = g + a * h_ref[...]
    h_ref[...] = h[q - 1:q, :]
    hn_ref[0] = h[q - 1:q, :]
    y_ref[...] = h * jax.nn.gelu(gate_ref[...])


def _gla_core(qv, kv, vv, log_f, st_ref, nh, dk, dv, q):
    qp = max(q, GLA_CHUNK)
    qv, kv, vv, log_f = (_pad_rows(t, qp) for t in (qv, kv, vv, log_f))
    causal = _tri(qp)
    b = _cumsum_rows(causal.astype(BF16), log_f)
    b_ref = b[qp // 2:qp // 2 + 1, :]
    bl = b[qp - 1:qp, :]
    qe = qv * jnp.exp(b - b_ref)
    ke = kv * jnp.exp(b_ref - b)
    qb = qv * jnp.exp(b)
    kb = kv * jnp.exp(bl - b)
    ebl = jnp.exp(bl)
    outs = []
    for h in range(nh):
        ks = slice(h * dk, (h + 1) * dk)
        vs = slice(h * dv, (h + 1) * dv)
        sc = jnp.where(causal, _dot_nt(qe[:, ks], ke[:, ks]), 0.0)
        s_h = st_ref[h]
        outs.append(_dot(sc, vv[:, vs]) + _dot(qb[:, ks], s_h))
        upd = _dot(kb[:, ks].T, vv[:, vs])
        e_col = jnp.broadcast_to(ebl[:, ks], (dk, dk)).T
        if dv != dk:
            e_col = jnp.concatenate([e_col] * (dv // dk), axis=1)
        st_ref[h] = s_h * e_col + upd
    return jnp.concatenate(outs, axis=1)[:q]


def _head_rms_gate(o, gate, g_tiled, nh, dv):
    outs = []
    for h in range(nh):
        oh = o[:, h * dv:(h + 1) * dv]
        outs.append(oh * lax.rsqrt(jnp.mean(oh * oh, -1, keepdims=True) + RMS_EPS))
    return jnp.concatenate(outs, axis=1) * g_tiled * _silu(gate)


def _gla_kernel(qg_ref, kg_ref, vg_ref, gg_ref, lr_ref, s0_ref, wg2_ref, bg2_ref, ng_ref,
                *rest, q, nc, has_prev):
    if has_prev:
        rest = rest[1:]
    y_ref, sn_ref, st_ref = rest
    c = pl.program_id(1)

    @pl.when(c == 0)
    def _():
        st_ref[...] = s0_ref[0]

    gl = _dot(lr_ref[...], wg2_ref[...]) + bg2_ref[...]
    log_f = -_softplus(-gl) / GLA_GATE_NORM
    o = _gla_core(qg_ref[...] * GLA_DK ** -0.5, kg_ref[...], vg_ref[...], log_f, st_ref,
                  GLA_HEADS, GLA_DK, GLA_DV, q)
    y_ref[...] = _head_rms_gate(o, gg_ref[...], ng_ref[...], GLA_HEADS, GLA_DV)

    @pl.when(c == nc - 1)
    def _():
        sn_ref[0] = st_ref[...]


def _hgrn_kernel(qh_ref, fh_ref, ih_ref, gh_ref, s0_ref, lbl_ref, ng_ref,
                 *rest, q, nc, has_prev):
    if has_prev:
        rest = rest[1:]
    y_ref, sn_ref, st_ref = rest
    c = pl.program_id(1)

    @pl.when(c == 0)
    def _():
        st_ref[...] = s0_ref[0]

    l0 = lbl_ref[0:1, :]
    l1 = lbl_ref[1:2, :]
    mx = jnp.maximum(l0, l1)
    e0 = jnp.exp(l0 - mx)
    e1 = jnp.exp(l1 - mx)
    g0 = e0 / (e0 + e1)
    g1 = e1 / (e0 + e1)
    lb = (g0 + g1) - g0
    fs = fh_ref[...]
    log_f = jnp.log(lb + (1.0 - lb) * jax.nn.sigmoid(fs))
    kh = (1.0 - lb) * jax.nn.sigmoid(-fs)
    o = _gla_core(_silu(qh_ref[...]), kh, ih_ref[...], log_f, st_ref,
                  HGRN_HEADS, HGRN_DK, HGRN_DV, q)
    y_ref[...] = _head_rms_gate(o, gh_ref[...], ng_ref[...], HGRN_HEADS, HGRN_DV)

    @pl.when(c == nc - 1)
    def _():
        sn_ref[0] = st_ref[...]


def _pad_cols(w, n):
    return jnp.pad(w, ((0, 0), (0, n - w.shape[1])))


def _prep_even(p, j):
    w = p['w_in_even'][j]
    z, xbc, dt, rgg, rgx = jnp.split(w, [1024, 2560, 2576, 3600], axis=1)
    w_in = jnp.concatenate([z, xbc[:, :1024], rgg, rgx, xbc[:, 1024:], dt], axis=1)
    w_in = _pad_cols(w_in, EVEN_COLS).astype(BF16)
    cw, cb = p['ssd_conv_w'][j], p['ssd_conv_b'][j][None]
    lane_pad = lambda v: _pad_cols(v[None], LANES)
    ssd = (cw[:, :SSD_DINNER], cw[:, SSD_DINNER:], cb[:, :SSD_DINNER], cb[:, SSD_DINNER:],
           lane_pad(p['ssd_dt_bias'][j]), lane_pad(p['ssd_a_log'][j]),
           jnp.repeat(p['ssd_d'][j], SSD_HEADDIM)[None], p['ssd_norm_g'][j][None])
    wax = jnp.concatenate([p['rg_wa'][j], p['rg_wx'][j]], axis=2).astype(BF16)
    bax = jnp.concatenate([p['rg_ba'][j].reshape(RG_BLOCKS, 1, RG_BW),
                           p['rg_bx'][j].reshape(RG_BLOCKS, 1, RG_BW)], axis=2)
    rg = (p['rg_conv_w'][j], p['rg_conv_b'][j][None], wax, bax, p['rg_lambda'][j][None])
    w_out = p['w_out_even'][j].astype(BF16)
    return w_in, ssd, rg, w_out[:SSD_DINNER], w_out[SSD_DINNER:]


def _prep_odd(p, j):
    w = p['w_in_odd'][j]
    main, lr, rest = jnp.split(w, [3072, 3088], axis=1)
    w_in = _pad_cols(jnp.concatenate([main, rest, lr], axis=1), ODD_COLS).astype(BF16)
    wg2 = jnp.pad(p['gla_wg2'][j], ((0, LANES - GLA_RANK), (0, 0))).astype(BF16)
    gla = (wg2, p['gla_bg2'][j][None], jnp.tile(p['gla_norm_g'][j], GLA_HEADS)[None])
    hgrn = (p['hgrn_lb_logits'], jnp.tile(p['hgrn_norm_g'][j], HGRN_HEADS)[None])
    w_out = p['w_out_odd'][j].astype(BF16)
    nv = GLA_HEADS * GLA_DV
    return w_in, gla, hgrn, w_out[:nv], w_out[nv:]


def _even_mixers(u, groups, ssd_p, rg_p):
    y_ssd = y_rg = None
    new_states = []
    for row_off, nseq, seqlen, q_ssd, q_rg, st, shared in groups:
        s_ssd, s_conv, s_rg, s_rgconv = st
        y_ssd, conv_n, ssd_n = _seq_call(
            _ssd_kernel, name="ssd", u=u, row_off=row_off, nseq=nseq, seqlen=seqlen, q=q_ssd,
            u_cols=[(E_Z, 1024), (E_XS, 1024), (E_BC, SSD_BC), (E_DT, LANES)],
            states=[s_conv, s_ssd], shared_init=shared, params=ssd_p, y_prev=y_ssd,
            state_out_shapes=[(CONV_W - 1, SSD_DINNER + SSD_BC), (SSD_DINNER, SSD_DSTATE)],
            scratch=[pltpu.VMEM((q_ssd + SUBLANES, SSD_DINNER), F32),
                     pltpu.VMEM((q_ssd + SUBLANES, SSD_BC), F32),
                     pltpu.VMEM((SSD_DINNER, SSD_DSTATE), F32)])
        y_rg, rgconv_n, rg_n = _seq_call(
            _rg_kernel, name="rglru", u=u, row_off=row_off, nseq=nseq, seqlen=seqlen, q=q_rg,
            u_cols=[(E_RGG, 1024), (E_RGX, 1024)],
            states=[s_rgconv, s_rg], shared_init=shared, params=rg_p, y_prev=y_rg,
            state_out_shapes=[(CONV_W - 1, RG_WIDTH), (1, RG_WIDTH)],
            scratch=[pltpu.VMEM((q_rg + SUBLANES, RG_WIDTH), F32), pltpu.VMEM((1, RG_WIDTH), F32)])
        new_states.append((ssd_n, conv_n, rg_n, rgconv_n))
    return y_ssd, y_rg, new_states


def _odd_mixers(u, groups, gla_p, hgrn_p):
    y_g = y_h = None
    new_states = []
    for row_off, nseq, seqlen, q, st, shared in groups:
        s_gla, s_hgrn = st
        y_g, gla_n = _seq_call(
            _gla_kernel, name="gla", u=u, row_off=row_off, nseq=nseq, seqlen=seqlen, q=q,
            u_cols=[(O_QG, 512), (O_KG, 512), (O_VG, 1024), (O_GG, 1024), (O_LR, LANES)],
            states=[s_gla], shared_init=shared, params=gla_p, y_prev=y_g,
            state_out_shapes=[(GLA_HEADS, GLA_DK, GLA_DV)],
            scratch=[pltpu.VMEM((GLA_HEADS, GLA_DK, GLA_DV), F32)])
        y_h, hgrn_n = _seq_call(
            _hgrn_kernel, name="hgrn", u=u, row_off=row_off, nseq=nseq, seqlen=seqlen, q=q,
            u_cols=[(O_QH, 1024), (O_FH, 1024), (O_IH, 1024), (O_GH, 1024)],
            states=[s_hgrn], shared_init=shared, params=hgrn_p, y_prev=y_h,
            state_out_shapes=[(HGRN_HEADS, HGRN_DK, HGRN_DV)],
            scratch=[pltpu.VMEM((HGRN_HEADS, HGRN_DK, HGRN_DV), F32)])
        new_states.append((gla_n, hgrn_n))
    return y_g, y_h, new_states


def kernel(x_prompt, x_sample, state_ssd, state_ssd_conv, state_rglru, state_rglru_conv, state_gla, state_hgrn, meta_tokens, w_in_even, ssd_conv_w, ssd_conv_b, ssd_dt_bias, ssd_a_log, ssd_d, ssd_norm_g, rg_conv_w, rg_conv_b, rg_wa, rg_ba, rg_wx, rg_bx, rg_lambda, w_out_even, w_in_odd, gla_wg2, gla_bg2, gla_norm_g, hgrn_lb_logits, hgrn_norm_g, w_out_odd, mlp_w1, mlp_w2, ln1_g, ln1_b, ln2_g, ln2_b):
    assert DEPTH == 2
    p = {'w_in_even': w_in_even, 'ssd_conv_w': ssd_conv_w, 'ssd_conv_b': ssd_conv_b,
         'ssd_dt_bias': ssd_dt_bias, 'ssd_a_log': ssd_a_log, 'ssd_d': ssd_d, 'ssd_norm_g': ssd_norm_g,
         'rg_conv_w': rg_conv_w, 'rg_conv_b': rg_conv_b, 'rg_wa': rg_wa, 'rg_ba': rg_ba, 'rg_wx': rg_wx,
         'rg_bx': rg_bx, 'rg_lambda': rg_lambda, 'w_out_even': w_out_even, 'w_in_odd': w_in_odd,
         'gla_wg2': gla_wg2, 'gla_bg2': gla_bg2, 'gla_norm_g': gla_norm_g, 'hgrn_lb_logits': hgrn_lb_logits,
         'hgrn_norm_g': hgrn_norm_g, 'w_out_odd': w_out_odd}
    bp, lp, d = x_prompt.shape
    bs, ls, _ = x_sample.shape
    tp, ts = bp * lp, bs * ls

    h_main = jnp.concatenate([x_prompt.reshape(tp, d), x_sample.reshape(ts, d)], axis=0)
    h_meta = meta_tokens.astype(F32)

    def dense_tail(h, ya, yb, wa, wb, layer, tag):
        h = _outproj_ln(h, ya, yb, wa, wb, ln1_g[layer][None], ln1_b[layer][None], "outproj_ln" + tag)
        return _mlp_ln(h, mlp_w1[layer].astype(BF16), mlp_w2[layer].astype(BF16),
                       ln2_g[layer][None], ln2_b[layer][None], "mlp_ln" + tag)

    w_in, ssd_p, rg_p, wo_a, wo_b = _prep_even(p, 0)
    u_meta = _in_proj(h_meta, w_in, "in_proj_meta")
    u_main = _in_proj(h_main, w_in, "in_proj")
    zeros = (jnp.zeros((1, SSD_DINNER, SSD_DSTATE), F32), jnp.zeros((1, CONV_W - 1, SSD_DINNER + SSD_BC), F32),
             jnp.zeros((1, 1, RG_WIDTH), F32), jnp.zeros((1, CONV_W - 1, RG_WIDTH), F32))
    ya_m, yb_m, (m_st,) = _even_mixers(u_meta, [(0, 1, N_META, N_META, N_META, zeros, False)], ssd_p, rg_p)
    s_st = (state_ssd[0].reshape(bs, SSD_DINNER, SSD_DSTATE), state_ssd_conv[0],
            state_rglru[0].reshape(bs, 1, RG_WIDTH), state_rglru_conv[0])
    ya, yb, (p_st, s_st) = _even_mixers(
        u_main, [(0, bp, lp, SSD_CHUNK, RG_CHUNK, m_st, True), (tp, bs, ls, ls, ls, s_st, False)], ssd_p, rg_p)
    h_meta = dense_tail(h_meta, ya_m, yb_m, wo_a, wo_b, 0, "_meta")
    h_main = dense_tail(h_main, ya, yb, wo_a, wo_b, 0, "")
    p_ssd, p_ssd_conv, p_rg, p_rg_conv = p_st
    s_ssd, s_ssd_conv, s_rg, s_rg_conv = s_st

    w_in, gla_p, hgrn_p, wo_a, wo_b = _prep_odd(p, 0)
    u_meta = _in_proj(h_meta, w_in, "in_proj_meta")
    u_main = _in_proj(h_main, w_in, "in_proj")
    zeros = (jnp.zeros((1, GLA_HEADS, GLA_DK, GLA_DV), F32), jnp.zeros((1, HGRN_HEADS, HGRN_DK, HGRN_DV), F32))
    ya_m, yb_m, (m_st,) = _odd_mixers(u_meta, [(0, 1, N_META, N_META, zeros, False)], gla_p, hgrn_p)
    ya, yb, (p_st, s_st) = _odd_mixers(
        u_main, [(0, bp, lp, GLA_CHUNK, m_st, True), (tp, bs, ls, ls, (state_gla[0], state_hgrn[0]), False)],
        gla_p, hgrn_p)
    h_main = dense_tail(h_main, ya, yb, wo_a, wo_b, 1, "")
    p_gla, p_hgrn = p_st
    s_gla, s_hgrn = s_st

    y_prompt = h_main[:tp].reshape(bp, lp, d)
    y_sample = h_main[tp:].reshape(bs, ls, d)
    ssd_shape = lambda n: (1, n, SSD_HEADS, SSD_HEADDIM, SSD_DSTATE)
    return (y_prompt, y_sample,
            p_ssd.reshape(ssd_shape(bp)), p_ssd_conv[None], p_rg.reshape(1, bp, RG_WIDTH), p_rg_conv[None],
            p_gla[None], p_hgrn[None],
            s_ssd.reshape(ssd_shape(bs)), s_ssd_conv[None], s_rg.reshape(1, bs, RG_WIDTH), s_rg_conv[None],
            s_gla[None], s_hgrn[None])
```
